```python
import jax, jax.numpy as jnp
from jax import lax
import numpy as np

D_MODEL = 1024
BATCH = 32
SEQ = 2048
DEPTH = 4

CHUNK = 64
Q_BLOCK = 128
EPS = 1e-6
NEG_INF = -1e30

MLA_HEADS = 8
QK_NOPE = 64
QK_ROPE = 32
V_HEAD = 64
Q_LORA = 256
KV_LORA = 128
ROPE_THETA = 10000.0
MLA_WIDTH = MLA_HEADS * V_HEAD

POOL_WINDOWS = (2, 4, 8, 16)
POOL_GROUPS = 4
POOL_GROUP_DIM = 64
POOL_WIDTH = POOL_GROUPS * POOL_GROUP_DIM
POOL_MAX_WIN = 16

SGU_BLOCK = 128
SGU_GROUPS = 4
SGU_GROUP_DIM = 64
SGU_WIDTH = SGU_GROUPS * SGU_GROUP_DIM

CONV_K = 3
CONV_WIDTH = 256

N_BRANCH = 4
D_FF = ((8 * D_MODEL + 3 * 256 - 1) // (3 * 256)) * 256

IN_SIZES = (Q_LORA, KV_LORA, QK_ROPE, POOL_WIDTH, SGU_WIDTH, SGU_WIDTH,
            CONV_WIDTH, CONV_WIDTH, CONV_WIDTH, N_BRANCH * D_MODEL)
IN_WIDTH = sum(IN_SIZES)

kernel_name = "hybrid_gated_mla_pool_sgu_conv_trunk"


def rmsnorm(x, g):
    xf = x.astype(jnp.float32)
    y = xf * lax.rsqrt(jnp.mean(xf * xf, axis=-1, keepdims=True) + EPS)
    return (y * g.astype(jnp.float32)).astype(x.dtype)


def split_cols(p):
    out = []
    o = 0
    for n in IN_SIZES:
        out.append(p[..., o:o + n])
        o += n
    return out


def rope_tables(seq, dtype):
    inv = ROPE_THETA ** (-jnp.arange(0, QK_ROPE, 2, dtype=jnp.float32) / QK_ROPE)
    ang = jnp.arange(seq, dtype=jnp.float32)[:, None] * inv[None, :]
    return jnp.cos(ang).astype(dtype), jnp.sin(ang).astype(dtype)


def apply_rope(x, cos, sin):
    half = x.shape[-1] // 2
    x1, x2 = x[..., :half], x[..., half:]
    return jnp.concatenate([x1 * cos - x2 * sin, x2 * cos + x1 * sin], axis=-1)


def mla_branch(c_q, c_kv, k_r, g_cq, g_ckv, w_uq, w_ukv, cos, sin):
    B, S, _ = c_q.shape
    q = (rmsnorm(c_q, g_cq) @ w_uq).reshape(B, S, MLA_HEADS, QK_NOPE + QK_ROPE)
    q_nope = q[..., :QK_NOPE]
    q_rope = apply_rope(q[..., QK_NOPE:], cos[:, None, :], sin[:, None, :])
    kv = (rmsnorm(c_kv, g_ckv) @ w_ukv).reshape(B, S, MLA_HEADS, QK_NOPE + V_HEAD)
    k_nope, v = kv[..., :QK_NOPE], kv[..., QK_NOPE:]
    k_rope = apply_rope(k_r, cos, sin)
    scale = (QK_NOPE + QK_ROPE) ** -0.5
    chunk_id = jnp.arange(S) // CHUNK
    outs = []
    for blk in range(S // Q_BLOCK):
        q0 = blk * Q_BLOCK
        kend = q0 + Q_BLOCK
        s = (jnp.einsum('bqhd,bkhd->bhqk', q_nope[:, q0:kend], k_nope[:, :kend])
             + jnp.einsum('bqhr,bkr->bhqk', q_rope[:, q0:kend], k_rope[:, :kend]))
        s = s.astype(jnp.float32) * scale
        mask = chunk_id[q0:kend, None] >= chunk_id[None, :kend]
        s = jnp.where(mask[None, None], s, NEG_INF)
        p = jax.nn.softmax(s, axis=-1).astype(v.dtype)
        outs.append(jnp.einsum('bhqk,bkhd->bqhd', p, v[:, :kend]))
    return jnp.concatenate(outs, axis=1).reshape(B, S, MLA_WIDTH)


def pool_branch(z, w_grp, scale):
    B, S, _ = z.shape
    zg = z.reshape(B, S, POOL_GROUPS, POOL_GROUP_DIM).astype(jnp.float32)
    csp = jnp.pad(jnp.cumsum(zg, axis=1), ((0, 0), (POOL_MAX_WIN, 0), (0, 0), (0, 0)))
    t = jnp.arange(S)
    outs = []
    for g, w in enumerate(POOL_WINDOWS):
        win_sum = (csp[:, POOL_MAX_WIN:POOL_MAX_WIN + S, g]
                   - csp[:, POOL_MAX_WIN - w:POOL_MAX_WIN - w + S, g])
        count = jnp.minimum(t + 1, w).astype(jnp.float32)
        outs.append(win_sum / count[None, :, None] - zg[:, :, g])
    pooled = jnp.stack(outs, axis=2).astype(z.dtype)
    mixed = jnp.einsum('bsgc,gcd->bsgd', pooled, w_grp)
    return mixed.reshape(B, S, POOL_WIDTH) * scale


def sgu_branch(u, v, g_v, w_s, b_s):
    B, S, _ = u.shape
    n = S // SGU_BLOCK
    vn = rmsnorm(v, g_v).reshape(B, n, SGU_BLOCK, SGU_GROUPS, SGU_GROUP_DIM)
    pos_chunk = jnp.arange(SGU_BLOCK) // CHUNK
    mask = pos_chunk[:, None] >= pos_chunk[None, :]
    w = jnp.where(mask[None], w_s, 0)
    mixed = jnp.einsum('gij,bnjgc->bnigc', w, vn) + b_s.T[None, None, :, :, None]
    return u * mixed.reshape(B, S, SGU_WIDTH)


def conv_branch(b_gate, c_gate, x_in, w_conv):
    z = c_gate * x_in
    y = lax.conv_general_dilated(z, w_conv, window_strides=(1,),
                                 padding=[(CONV_K - 1, 0)],
                                 dimension_numbers=('NWC', 'WIO', 'NWC'),
                                 feature_group_count=CONV_WIDTH)
    return b_gate * y


def setup_inputs(seed: int = 0) -> dict:
    key = jax.random.key(seed)
    ks = iter(jax.random.split(key, 32))

    def nrm(shape, fan_in):
        return jax.random.normal(next(ks), shape, jnp.float32) * (fan_in ** -0.5)

    def gain(shape):
        return 1.0 + 0.1 * jax.random.normal(next(ks), shape, jnp.float32)

    L = DEPTH
    return {
        "x": jax.random.normal(next(ks), (BATCH, SEQ, D_MODEL), jnp.float32),
        "w_in": nrm((L, D_MODEL, IN_WIDTH), D_MODEL),
        "g_pre_mix": gain((L, D_MODEL)),
        "g_cq": gain((L, Q_LORA)),
        "g_ckv": gain((L, KV_LORA)),
        "w_uq": nrm((L, Q_LORA, MLA_HEADS * (QK_NOPE + QK_ROPE)), Q_LORA),
        "w_ukv": nrm((L, KV_LORA, MLA_HEADS * (QK_NOPE + V_HEAD)), KV_LORA),
        "pool_w": nrm((L, POOL_GROUPS, POOL_GROUP_DIM, POOL_GROUP_DIM), POOL_GROUP_DIM),
        "pool_scale": gain((L, POOL_WIDTH)),
        "g_sgu_v": gain((L, SGU_WIDTH)),
        "sgu_w": nrm((L, SGU_GROUPS, SGU_BLOCK, SGU_BLOCK), SGU_BLOCK),
        "sgu_b": gain((L, SGU_GROUPS, SGU_BLOCK)),
        "conv_w": nrm((L, CONV_K, 1, CONV_WIDTH), CONV_K),
        "w_br_a": nrm((L, MLA_WIDTH, D_MODEL), MLA_WIDTH),
        "w_br_b": nrm((L, POOL_WIDTH, D_MODEL), POOL_WIDTH),
        "w_br_c": nrm((L, SGU_WIDTH, D_MODEL), SGU_WIDTH),
        "w_br_d": nrm((L, CONV_WIDTH, D_MODEL), CONV_WIDTH),
        "w_out": nrm((L, D_MODEL, D_MODEL), D_MODEL),
        "g_post_mix": gain((L, D_MODEL)),
        "g_pre_ffn": gain((L, D_MODEL)),
        "w_ffn_gate": nrm((L, D_MODEL, D_FF), D_MODEL),
        "w_ffn_up": nrm((L, D_MODEL, D_FF), D_MODEL),
        "w_ffn_down": nrm((L, D_FF, D_MODEL), D_FF),
        "g_post_ffn": gain((L, D_MODEL)),
    }


def reference(x, w_in, g_pre_mix, g_cq, g_ckv, w_uq, w_ukv, pool_w, pool_scale,
              g_sgu_v, sgu_w, sgu_b, conv_w, w_br_a, w_br_b, w_br_c, w_br_d,
              w_out, g_post_mix, g_pre_ffn, w_ffn_gate, w_ffn_up, w_ffn_down,
              g_post_ffn):
    B, S, D = x.shape
    cos, sin = rope_tables(S, x.dtype)
    for l in range(DEPTH):
        h = rmsnorm(x, g_pre_mix[l])
        (c_q, c_kv, k_r, p_in, s_u, s_v, cv_b, cv_c, cv_x,
         gate_logits) = split_cols(h @ w_in[l])
        y_a = mla_branch(c_q, c_kv, k_r, g_cq[l], g_ckv[l], w_uq[l], w_ukv[l], cos, sin) @ w_br_a[l]
        y_b = pool_branch(p_in, pool_w[l], pool_scale[l]) @ w_br_b[l]
        y_c = sgu_branch(s_u, s_v, g_sgu_v[l], sgu_w[l], sgu_b[l]) @ w_br_c[l]
        y_d = conv_branch(cv_b, cv_c, cv_x, conv_w[l]) @ w_br_d[l]
        gates = jax.nn.sigmoid(gate_logits.astype(jnp.float32)).astype(x.dtype)
        gates = gates.reshape(B, S, N_BRANCH, D)
        merged = (gates[:, :, 0] * y_a + gates[:, :, 1] * y_b
                  + gates[:, :, 2] * y_c + gates[:, :, 3] * y_d)
        x = x + rmsnorm(merged @ w_out[l], g_post_mix[l])
        h = rmsnorm(x, g_pre_ffn[l])
        f = (jax.nn.silu(h @ w_ffn_gate[l]) * (h @ w_ffn_up[l])) @ w_ffn_down[l]
        x = x + rmsnorm(f, g_post_ffn[l])
    return x
```

```python
import functools

import jax
import jax.numpy as jnp
from jax import lax
from jax.experimental import pallas as pl
from jax.experimental.pallas import tpu as pltpu

D_MODEL = 1024
SEQ = 2048
DEPTH = 4
CHUNK = 64
EPS = 1e-6
NEG_INF = -1e30

MLA_HEADS = 8
QK_NOPE = 64
QK_ROPE = 32
V_HEAD = 64
Q_LORA = 256
KV_LORA = 128
ROPE_THETA = 10000.0
MLA_WIDTH = MLA_HEADS * V_HEAD

POOL_WINDOWS = (2, 4, 8, 16)
POOL_GROUP_DIM = 64
POOL_WIDTH = 256
POOL_MAX_WIN = 16

SGU_BLOCK = 128
SGU_GROUPS = 4
SGU_GROUP_DIM = 64
SGU_WIDTH = 256

CONV_K = 3
CONV_WIDTH = 256
CONV_HALO = 8

N_BRANCH = 4
D_FF = 2816

LANES = 128
HEAD_PAD = LANES

_OFF_CQ = 0
_OFF_CKV = _OFF_CQ + Q_LORA
_OFF_KR = _OFF_CKV + KV_LORA
_OFF_KRSW = _OFF_KR + HEAD_PAD
_OFF_POOL = _OFF_KRSW + HEAD_PAD
_OFF_SU = _OFF_POOL + POOL_WIDTH
_OFF_SV = _OFF_SU + SGU_WIDTH
_OFF_CVB = _OFF_SV + SGU_WIDTH
_OFF_CVC = _OFF_CVB + CONV_WIDTH
_OFF_CVX = _OFF_CVC + CONV_WIDTH
SMALL_WIDTH = _OFF_CVX + CONV_WIDTH

MIX_TILE = 256
FFN_TILE = 256
VMEM_LIMIT_BYTES = 56 * 1024 * 1024

_BF16 = jnp.bfloat16
_F32 = jnp.float32


def _rms(x, g):
    return x * lax.rsqrt(jnp.mean(x * x, axis=-1, keepdims=True) + EPS) * g


def _dot(a, b):
    return jnp.dot(a, b, preferred_element_type=_F32)


def _dot_nt(a, b):
    return lax.dot_general(a, b, (((1,), (1,)), ((), ())), preferred_element_type=_F32)


def _sigmoid(x):
    return 1.0 / (1.0 + jnp.exp(-x))


def _mix_kernel(x_ref, ct_ref, st_ref, g_pre_ref, w_small_ref, w_gate_ref,
                g_cq_ref, g_ckv_ref, w_uq_ref, w_ukv_ref, pool_w_ref, pool_scale_ref,
                g_sgu_ref, sgu_w_ref, sgu_b_ref, conv_w_ref,
                w_bra_ref, w_brb_ref, w_brc_ref, w_brd_ref, w_out_ref, g_post_ref,
                o_ref, k_scr, v_scr, pool_halo, conv_halo):
    T = MIX_TILE
    t = pl.program_id(1)
    row0 = pl.multiple_of(t * T, T)

    @pl.when(t == 0)
    def _():
        pool_halo[...] = jnp.zeros_like(pool_halo)
        conv_halo[...] = jnp.zeros_like(conv_halo)

    x = x_ref[...]
    hb = _rms(x, g_pre_ref[...]).astype(_BF16)
    small = _dot(hb, w_small_ref[...])

    ct = ct_ref[...]
    st = st_ref[...]

    cqn = _rms(small[:, _OFF_CQ:_OFF_CQ + Q_LORA], g_cq_ref[...]).astype(_BF16)
    qall = _dot(cqn, w_uq_ref[...])
    ckvn = _rms(small[:, _OFF_CKV:_OFF_CKV + KV_LORA], g_ckv_ref[...]).astype(_BF16)
    kvall = _dot(ckvn, w_ukv_ref[...])
    k_rope = (small[:, _OFF_KR:_OFF_KR + HEAD_PAD] * ct
              + small[:, _OFF_KRSW:_OFF_KRSW + HEAD_PAD] * st)

    for h in range(MLA_HEADS):
        k_scr[h, pl.ds(row0, T), :] = (kvall[:, h * HEAD_PAD:(h + 1) * HEAD_PAD] + k_rope).astype(_BF16)
    v_off = MLA_HEADS * HEAD_PAD
    v_scr[pl.ds(row0, T), :] = kvall[:, v_off:v_off + MLA_WIDTH].astype(_BF16)

    scale = (QK_NOPE + QK_ROPE) ** -0.5
    r_chunk = lax.broadcasted_iota(jnp.int32, (T, T), 0) // CHUNK
    c_chunk = lax.broadcasted_iota(jnp.int32, (T, T), 1) // CHUNK
    diag_mask = r_chunk >= c_chunk
    lane = lax.broadcasted_iota(jnp.int32, (T, LANES), 1)

    def attend(h):
        q_off = h * HEAD_PAD
        q_h = ((qall[:, q_off:q_off + HEAD_PAD] * ct
                + qall[:, v_off + q_off:v_off + q_off + HEAD_PAD] * st) * scale).astype(_BF16)
        pair = (h // 2) * LANES

        s = _dot_nt(q_h, k_scr[h, pl.ds(row0, T), :])
        s = jnp.where(diag_mask, s, NEG_INF)
        m = jnp.max(s, axis=-1, keepdims=True)
        p = jnp.exp(s - m)
        l = jnp.sum(p, axis=-1, keepdims=True)
        acc = _dot(p.astype(_BF16), v_scr[pl.ds(row0, T), pair:pair + LANES])

        def body(kb, carry):
            m, l, acc = carry
            k0 = pl.multiple_of(kb * T, T)
            s = _dot_nt(q_h, k_scr[h, pl.ds(k0, T), :])
            m_new = jnp.maximum(m, jnp.max(s, axis=-1, keepdims=True))
            alpha = jnp.exp(m - m_new)
            p = jnp.exp(s - m_new)
            l = alpha * l + jnp.sum(p, axis=-1, keepdims=True)
            acc = alpha * acc + _dot(p.astype(_BF16), v_scr[pl.ds(k0, T), pair:pair + LANES])
            return m_new, l, acc

        m, l, acc = lax.fori_loop(0, t, body, (m, l, acc))
        return acc / l

    o_pairs = []
    for j in range(MLA_HEADS // 2):
        o_pairs.append(jnp.where(lane < V_HEAD, attend(2 * j), attend(2 * j + 1)))
    o_a = jnp.concatenate(o_pairs, axis=-1).astype(_BF16)

    merged = _sigmoid(_dot(hb, w_gate_ref[:, 0:D_MODEL])) * _dot(o_a, w_bra_ref[...])

    z = small[:, _OFF_POOL:_OFF_POOL + POOL_WIDTH]
    zext = jnp.concatenate([pool_halo[...], z], axis=0)
    pool_halo[...] = z[T - POOL_MAX_WIN:, :]
    s2 = zext + pltpu.roll(zext, 1, 0)
    s4 = s2 + pltpu.roll(s2, 2, 0)
    s8 = s4 + pltpu.roll(s4, 4, 0)
    s16 = s8 + pltpu.roll(s8, 8, 0)
    lane_p = lax.broadcasted_iota(jnp.int32, (T, POOL_WIDTH), 1)
    grp = lane_p // POOL_GROUP_DIM
    win_sum = jnp.where(grp == 0, s2[POOL_MAX_WIN:],
                        jnp.where(grp == 1, s4[POOL_MAX_WIN:],
                                  jnp.where(grp == 2, s8[POOL_MAX_WIN:], s16[POOL_MAX_WIN:])))
    win = jnp.where(grp == 0, POOL_WINDOWS[0],
                    jnp.where(grp == 1, POOL_WINDOWS[1],
                              jnp.where(grp == 2, POOL_WINDOWS[2], POOL_WINDOWS[3])))
    pos = row0 + lax.broadcasted_iota(jnp.int32, (T, POOL_WIDTH), 0)
    count = jnp.minimum(pos + 1, win).astype(_F32)
    pooled = (win_sum / count - z).astype(_BF16)
    o_b = (_dot(pooled, pool_w_ref[...]) * pool_scale_ref[...]).astype(_BF16)
    merged += _sigmoid(_dot(hb, w_gate_ref[:, D_MODEL:2 * D_MODEL])) * _dot(o_b, w_brb_ref[...])

    s_u = small[:, _OFF_SU:_OFF_SU + SGU_WIDTH]
    vn = _rms(small[:, _OFF_SV:_OFF_SV + SGU_WIDTH], g_sgu_ref[...])
    bi = lax.broadcasted_iota(jnp.int32, (SGU_BLOCK, SGU_BLOCK), 0) // CHUNK
    bj = lax.broadcasted_iota(jnp.int32, (SGU_BLOCK, SGU_BLOCK), 1) // CHUNK
    w_cat = jnp.concatenate(
        [jnp.where(bi >= bj, sgu_w_ref[g], 0.0) for g in range(SGU_GROUPS)], axis=-1
    ).astype(_BF16)
    lane_g = lax.broadcasted_iota(jnp.int32, (SGU_BLOCK, SGU_WIDTH), 1) // SGU_GROUP_DIM
    mixed = []
    for n in range(T // SGU_BLOCK):
        vb = vn[n * SGU_BLOCK:(n + 1) * SGU_BLOCK, :]
        rhs = jnp.concatenate(
            [jnp.where(lane_g == g, vb, 0.0) for g in range(SGU_GROUPS)], axis=0
        ).astype(_BF16)
        mixed.append(_dot(w_cat, rhs) + sgu_b_ref[...])
    o_c = (s_u * jnp.concatenate(mixed, axis=0)).astype(_BF16)
    merged += _sigmoid(_dot(hb, w_gate_ref[:, 2 * D_MODEL:3 * D_MODEL])) * _dot(o_c, w_brc_ref[...])

    zc = small[:, _OFF_CVC:_OFF_CVC + CONV_WIDTH] * small[:, _OFF_CVX:_OFF_CVX + CONV_WIDTH]
    zcext = jnp.concatenate([conv_halo[...], zc], axis=0)
    conv_halo[...] = zc[T - CONV_HALO:, :]
    y = (conv_w_ref[0:1, :] * pltpu.roll(zcext, 2, 0)[CONV_HALO:]
         + conv_w_ref[1:2, :] * pltpu.roll(zcext, 1, 0)[CONV_HALO:]
         + conv_w_ref[2:3, :] * zc)
    o_d = (small[:, _OFF_CVB:_OFF_CVB + CONV_WIDTH] * y).astype(_BF16)
    merged += _sigmoid(_dot(hb, w_gate_ref[:, 3 * D_MODEL:4 * D_MODEL])) * _dot(o_d, w_brd_ref[...])

    out = _dot(merged.astype(_BF16), w_out_ref[...])
    o_ref[...] = x + _rms(out, g_post_ref[...])


def _const_spec(shape):
    nd = len(shape)
    return pl.BlockSpec(shape, lambda b, t: (0,) * nd, pipeline_mode=pl.Buffered(1))


def _mix_layer(x, ct, st, p):
    B, S, D = x.shape
    T = MIX_TILE
    weights = [p["g_pre_mix"], p["w_small"], p["w_gate"], p["g_cq"], p["g_ckv"], p["w_uq"],
               p["w_ukv"], p["pool_w"], p["pool_scale"], p["g_sgu_v"], p["sgu_w"], p["sgu_b"],
               p["conv_w"], p["w_br_a"], p["w_br_b"], p["w_br_c"], p["w_br_d"], p["w_out"],
               p["g_post_mix"]]
    x_spec = pl.BlockSpec((None, T, D), lambda b, t: (b, t, 0))
    tab_spec = pl.BlockSpec((T, LANES), lambda b, t: (t, 0))
    return pl.pallas_call(
        _mix_kernel,
        out_shape=jax.ShapeDtypeStruct(x.shape, x.dtype),
        grid=(B, S // T),
        in_specs=[x_spec, tab_spec, tab_spec] + [_const_spec(w.shape) for w in weights],
        out_specs=x_spec,
        scratch_shapes=[
            pltpu.VMEM((MLA_HEADS, S, HEAD_PAD), _BF16),
            pltpu.VMEM((S, MLA_WIDTH), _BF16),
            pltpu.VMEM((POOL_MAX_WIN, POOL_WIDTH), _F32),
            pltpu.VMEM((CONV_HALO, CONV_WIDTH), _F32),
        ],
        compiler_params=pltpu.CompilerParams(
            dimension_semantics=("arbitrary", "arbitrary"),
            vmem_limit_bytes=VMEM_LIMIT_BYTES),
        name="mix_sublayer",
    )(x, ct, st, *weights)


def _ffn_kernel(x_ref, g_pre_ref, w_gate_ref, w_up_ref, w_down_ref, g_post_ref, o_ref):
    x = x_ref[...]
    hb = _rms(x, g_pre_ref[...]).astype(_BF16)
    a = _dot(hb, w_gate_ref[...])
    u = _dot(hb, w_up_ref[...])
    f = (a * _sigmoid(a) * u).astype(_BF16)
    o_ref[...] = x + _rms(_dot(f, w_down_ref[...]), g_post_ref[...])


def _ffn_layer(x2, p):
    N, D = x2.shape
    T = FFN_TILE
    weights = [p["g_pre_ffn"], p["w_ffn_gate"], p["w_ffn_up"], p["w_ffn_down"], p["g_post_ffn"]]
    x_spec = pl.BlockSpec((T, D), lambda i: (i, 0))

    def const_spec(shape):
        nd = len(shape)
        return pl.BlockSpec(shape, lambda i: (0,) * nd, pipeline_mode=pl.Buffered(1))

    return pl.pallas_call(
        _ffn_kernel,
        out_shape=jax.ShapeDtypeStruct(x2.shape, x2.dtype),
        grid=(N // T,),
        in_specs=[x_spec] + [const_spec(w.shape) for w in weights],
        out_specs=x_spec,
        compiler_params=pltpu.CompilerParams(
            dimension_semantics=("arbitrary",),
            vmem_limit_bytes=VMEM_LIMIT_BYTES),
        name="ffn_sublayer",
    )(x2, *weights)


def _rope_tables(seq):
    inv = ROPE_THETA ** (-jnp.arange(0, QK_ROPE, 2, dtype=_F32) / QK_ROPE)
    ang = jnp.arange(seq, dtype=_F32)[:, None] * inv[None, :]
    cos, sin = jnp.cos(ang), jnp.sin(ang)
    ones = jnp.ones((seq, QK_NOPE), _F32)
    zeros_n = jnp.zeros((seq, QK_NOPE), _F32)
    zeros_p = jnp.zeros((seq, HEAD_PAD - QK_NOPE - QK_ROPE), _F32)
    ct = jnp.concatenate([ones, cos, cos, zeros_p], axis=-1)
    st = jnp.concatenate([zeros_n, -sin, sin, zeros_p], axis=-1)
    return ct, st


def _swap_halves(w):
    half = w.shape[-1] // 2
    return jnp.concatenate([w[..., half:], w[..., :half]], axis=-1)


def _prepare(w_in, g_pre_mix, g_cq, g_ckv, w_uq, w_ukv, pool_w, pool_scale, g_sgu_v, sgu_w,
             sgu_b, conv_w, w_br_a, w_br_b, w_br_c, w_br_d, w_out, g_post_mix, g_pre_ffn,
             w_ffn_gate, w_ffn_up, w_ffn_down, g_post_ffn):
    L = w_in.shape[0]
    pad = HEAD_PAD - QK_NOPE - QK_ROPE

    o = Q_LORA + KV_LORA
    kr = w_in[..., o:o + QK_ROPE]
    zn = jnp.zeros(kr.shape[:-1] + (QK_NOPE,), kr.dtype)
    zp = jnp.zeros(kr.shape[:-1] + (pad,), kr.dtype)
    n_small = o + QK_ROPE
    n_rest = n_small + POOL_WIDTH + 2 * SGU_WIDTH + 3 * CONV_WIDTH
    w_small = jnp.concatenate(
        [w_in[..., :o], zn, kr, zp, zn, _swap_halves(kr), zp, w_in[..., n_small:n_rest]], axis=-1)
    assert w_small.shape[-1] == SMALL_WIDTH
    w_gate = w_in[..., n_rest:]

    uq = w_uq.reshape(L, Q_LORA, MLA_HEADS, QK_NOPE + QK_ROPE)
    uq_n, uq_r = uq[..., :QK_NOPE], uq[..., QK_NOPE:]
    zq = jnp.zeros(uq_r.shape[:-1] + (pad,), uq.dtype)
    q_heads = jnp.concatenate([uq_n, uq_r, zq], axis=-1).reshape(L, Q_LORA, MLA_HEADS * HEAD_PAD)
    q_swap = jnp.concatenate([jnp.zeros_like(uq_n), _swap_halves(uq_r), zq], axis=-1
                             ).reshape(L, Q_LORA, MLA_HEADS * HEAD_PAD)
    w_uq_all = jnp.concatenate([q_heads, q_swap], axis=-1)

    ukv = w_ukv.reshape(L, KV_LORA, MLA_HEADS, QK_NOPE + V_HEAD)
    uk, uv = ukv[..., :QK_NOPE], ukv[..., QK_NOPE:]
    k_heads = jnp.concatenate([uk, jnp.zeros(uk.shape[:-1] + (HEAD_PAD - QK_NOPE,), uk.dtype)],
                              axis=-1).reshape(L, KV_LORA, MLA_HEADS * HEAD_PAD)
    w_ukv_all = jnp.concatenate([k_heads, uv.reshape(L, KV_LORA, MLA_WIDTH)], axis=-1)

    pool_bd = jnp.zeros((L, POOL_WIDTH, POOL_WIDTH), pool_w.dtype)
    for g in range(len(POOL_WINDOWS)):
        sl = slice(g * POOL_GROUP_DIM, (g + 1) * POOL_GROUP_DIM)
        pool_bd = pool_bd.at[:, sl, sl].set(pool_w[:, g])

    sgu_bias = jnp.repeat(jnp.swapaxes(sgu_b, 1, 2), SGU_GROUP_DIM, axis=-1)

    def row(g):
        return g[:, None, :]

    bf = lambda w: w.astype(_BF16)
    stacked = {
        "g_pre_mix": row(g_pre_mix), "w_small": bf(w_small), "w_gate": bf(w_gate),
        "g_cq": row(g_cq), "g_ckv": row(g_ckv), "w_uq": bf(w_uq_all), "w_ukv": bf(w_ukv_all),
        "pool_w": bf(pool_bd), "pool_scale": row(pool_scale), "g_sgu_v": row(g_sgu_v),
        "sgu_w": sgu_w, "sgu_b": sgu_bias, "conv_w": conv_w[:, :, 0, :],
        "w_br_a": bf(w_br_a), "w_br_b": bf(w_br_b), "w_br_c": bf(w_br_c), "w_br_d": bf(w_br_d),
        "w_out": bf(w_out), "g_post_mix": row(g_post_mix), "g_pre_ffn": row(g_pre_ffn),
        "w_ffn_gate": bf(w_ffn_gate), "w_ffn_up": bf(w_ffn_up), "w_ffn_down": bf(w_ffn_down),
        "g_post_ffn": row(g_post_ffn),
    }
    return [{k: v[l] for k, v in stacked.items()} for l in range(L)]


def kernel(x, w_in, g_pre_mix, g_cq, g_ckv, w_uq, w_ukv, pool_w, pool_scale, g_sgu_v, sgu_w, sgu_b, conv_w, w_br_a, w_br_b, w_br_c, w_br_d, w_out, g_post_mix, g_pre_ffn, w_ffn_gate, w_ffn_up, w_ffn_down, g_post_ffn):
    B, S, D = x.shape
    layers = _prepare(w_in, g_pre_mix, g_cq, g_ckv, w_uq, w_ukv, pool_w, pool_scale, g_sgu_v,
                      sgu_w, sgu_b, conv_w, w_br_a, w_br_b, w_br_c, w_br_d, w_out, g_post_mix,
                      g_pre_ffn, w_ffn_gate, w_ffn_up, w_ffn_down, g_post_ffn)
    ct, st = _rope_tables(S)
    for p in layers:
        x = _mix_layer(x, ct, st, p)
        x = _ffn_layer(x.reshape(B * S, D), p).reshape(B, S, D)
    return x
```

```python
import functools

import jax
import jax.numpy as jnp
from jax import lax
from jax.experimental import pallas as pl
from jax.experimental.pallas import tpu as pltpu

D_MODEL = 1024
SEQ = 2048
DEPTH = 4
CHUNK = 64
EPS = 1e-6
NEG_INF = -1e30

MLA_HEADS = 8
QK_NOPE = 64
QK_ROPE = 32
V_HEAD = 64
Q_LORA = 256
KV_LORA = 128
ROPE_THETA = 10000.0
MLA_WIDTH = MLA_HEADS * V_HEAD

POOL_WINDOWS = (2, 4, 8, 16)
POOL_GROUP_DIM = 64
POOL_WIDTH = 256
POOL_MAX_WIN = 16

SGU_BLOCK = 128
SGU_GROUPS = 4
SGU_GROUP_DIM = 64
SGU_WIDTH = 256

CONV_K = 3
CONV_WIDTH = 256
CONV_HALO = 8

N_BRANCH = 4
D_FF = 2816

LANES = 128
HEAD_PAD = LANES

_OFF_CQ = 0
_OFF_CKV = _OFF_CQ + Q_LORA
_OFF_KR = _OFF_CKV + KV_LORA
_OFF_KRSW = _OFF_KR + HEAD_PAD
_OFF_POOL = _OFF_KRSW + HEAD_PAD
_OFF_SU = _OFF_POOL + POOL_WIDTH
_OFF_SV = _OFF_SU + SGU_WIDTH
_OFF_CVB = _OFF_SV + SGU_WIDTH
_OFF_CVC = _OFF_CVB + CONV_WIDTH
_OFF_CVX = _OFF_CVC + CONV_WIDTH
SMALL_WIDTH = _OFF_CVX + CONV_WIDTH

MIX_TILE = 256
FFN_TILE = 256
VMEM_LIMIT_BYTES = 56 * 1024 * 1024

_BF16 = jnp.bfloat16
_F32 = jnp.float32


def _rms(x, g):
    return x * lax.rsqrt(jnp.mean(x * x, axis=-1, keepdims=True) + EPS) * g


def _dot(a, b):
    return jnp.dot(a, b, preferred_element_type=_F32)


def _dot_nt(a, b):
    return lax.dot_general(a, b, (((1,), (1,)), ((), ())), preferred_element_type=_F32)


def _sigmoid(x):
    return 0.5 * jnp.tanh(0.5 * x) + 0.5


def _mix_kernel(x_ref, ct_ref, st_ref, g_pre_ref, w_small_ref, w_gate_ref,
                g_cq_ref, g_ckv_ref, w_uq_ref, w_uk_ref, w_uvt_ref, pool_w_ref, pool_scale_ref,
                g_sgu_ref, sgu_w_ref, sgu_b_ref, conv_w_ref,
                w_bra_ref, w_brb_ref, w_brc_ref, w_brd_ref, w_out_ref, g_post_ref,
                o_ref, k_scr, vt_scr, qt_scr, s_scr, m_scr, acc_scr, pool_halo, conv_halo):
    T = MIX_TILE
    t = pl.program_id(1)
    row0 = pl.multiple_of(t * T, T)

    @pl.when(t == 0)
    def _():
        pool_halo[...] = jnp.zeros_like(pool_halo)
        conv_halo[...] = jnp.zeros_like(conv_halo)

    x = x_ref[...]
    hb = _rms(x, g_pre_ref[...]).astype(_BF16)
    small = _dot(hb, w_small_ref[...])

    ct = ct_ref[...]
    st = st_ref[...]

    cqn = _rms(small[:, _OFF_CQ:_OFF_CQ + Q_LORA], g_cq_ref[...]).astype(_BF16)
    qall = _dot(cqn, w_uq_ref[...])
    ckvn_f = _rms(small[:, _OFF_CKV:_OFF_CKV + KV_LORA], g_ckv_ref[...])
    ckvn_t = ckvn_f.T.astype(_BF16)
    kall = _dot(ckvn_f.astype(_BF16), w_uk_ref[...])
    vt_all = _dot(w_uvt_ref[...], ckvn_t)
    k_rope = (small[:, _OFF_KR:_OFF_KR + HEAD_PAD] * ct
              + small[:, _OFF_KRSW:_OFF_KRSW + HEAD_PAD] * st)

    q_sw = MLA_HEADS * HEAD_PAD
    scale = (QK_NOPE + QK_ROPE) ** -0.5
    sub = lax.broadcasted_iota(jnp.int32, (HEAD_PAD, T), 0)
    for h in range(MLA_HEADS):
        hs = slice(h * HEAD_PAD, (h + 1) * HEAD_PAD)
        k_scr[t, h] = (kall[:, hs] + k_rope).astype(_BF16)
        vt_scr[t, h] = jnp.where(sub == V_HEAD, 1.0, vt_all[hs, :]).astype(_BF16)
        q_h = (qall[:, hs] * ct + qall[:, q_sw + h * HEAD_PAD:q_sw + (h + 1) * HEAD_PAD] * st) * scale
        qt_scr[h] = q_h.T.astype(_BF16)

    m_scr[...] = jnp.full(m_scr.shape, NEG_INF, _F32)
    acc_scr[...] = jnp.zeros(acc_scr.shape, _F32)
    for h in range(MLA_HEADS):
        s_scr[h] = _dot(k_scr[0, h], qt_scr[h])

    def online_update(h, s, kb):
        m_old = m_scr[h]
        m_new = jnp.maximum(m_old, jnp.max(s, axis=0, keepdims=True))
        m_scr[h] = m_new
        p = jnp.exp(s - m_new).astype(_BF16)
        acc_scr[h] = jnp.exp(m_old - m_new) * acc_scr[h] + _dot(vt_scr[kb, h], p)

    def kv_block(kb, carry):
        for h in range(MLA_HEADS):
            online_update(h, s_scr[h], kb)
            s_scr[h] = _dot(k_scr[kb + 1, h], qt_scr[h])
        return carry

    lax.fori_loop(0, t, kv_block, 0)

    k_chunk = lax.broadcasted_iota(jnp.int32, (T, T), 0) // CHUNK
    q_chunk = lax.broadcasted_iota(jnp.int32, (T, T), 1) // CHUNK
    diag_mask = q_chunk >= k_chunk
    for h in range(MLA_HEADS):
        online_update(h, jnp.where(diag_mask, s_scr[h], NEG_INF), t)

    o_heads = []
    for h in range(MLA_HEADS):
        acc = acc_scr[h]
        o_heads.append(acc[:V_HEAD, :] / acc[V_HEAD:V_HEAD + 1, :])
    o_a = jnp.concatenate(o_heads, axis=0).T.astype(_BF16)

    merged = _sigmoid(_dot(hb, w_gate_ref[:, 0:D_MODEL])) * _dot(o_a, w_bra_ref[...])

    z = small[:, _OFF_POOL:_OFF_POOL + POOL_WIDTH]
    zext = jnp.concatenate([pool_halo[...], z], axis=0)
    pool_halo[...] = z[T - POOL_MAX_WIN:, :]
    s2 = zext + pltpu.roll(zext, 1, 0)
    s4 = s2 + pltpu.roll(s2, 2, 0)
    s8 = s4 + pltpu.roll(s4, 4, 0)
    s16 = s8 + pltpu.roll(s8, 8, 0)
    lane_p = lax.broadcasted_iota(jnp.int32, (T, POOL_WIDTH), 1)
    grp = lane_p // POOL_GROUP_DIM
    win_sum = jnp.where(grp == 0, s2[POOL_MAX_WIN:],
                        jnp.where(grp == 1, s4[POOL_MAX_WIN:],
                                  jnp.where(grp == 2, s8[POOL_MAX_WIN:], s16[POOL_MAX_WIN:])))
    win = jnp.where(grp == 0, POOL_WINDOWS[0],
                    jnp.where(grp == 1, POOL_WINDOWS[1],
                              jnp.where(grp == 2, POOL_WINDOWS[2], POOL_WINDOWS[3])))
    pos = row0 + lax.broadcasted_iota(jnp.int32, (T, POOL_WIDTH), 0)
    count = jnp.minimum(pos + 1, win).astype(_F32)
    pooled = (win_sum / count - z).astype(_BF16)
    o_b = (_dot(pooled, pool_w_ref[...]) * pool_scale_ref[...]).astype(_BF16)
    merged += _sigmoid(_dot(hb, w_gate_ref[:, D_MODEL:2 * D_MODEL])) * _dot(o_b, w_brb_ref[...])

    s_u = small[:, _OFF_SU:_OFF_SU + SGU_WIDTH]
    vn = _rms(small[:, _OFF_SV:_OFF_SV + SGU_WIDTH], g_sgu_ref[...])
    bi = lax.broadcasted_iota(jnp.int32, (SGU_BLOCK, SGU_BLOCK), 0) // CHUNK
    bj = lax.broadcasted_iota(jnp.int32, (SGU_BLOCK, SGU_BLOCK), 1) // CHUNK
    w_cat = jnp.concatenate(
        [jnp.where(bi >= bj, sgu_w_ref[g], 0.0) for g in range(SGU_GROUPS)], axis=-1
    ).astype(_BF16)
    lane_g = lax.broadcasted_iota(jnp.int32, (SGU_BLOCK, SGU_WIDTH), 1) // SGU_GROUP_DIM
    mixed = []
    for n in range(T // SGU_BLOCK):
        vb = vn[n * SGU_BLOCK:(n + 1) * SGU_BLOCK, :]
        rhs = jnp.concatenate(
            [jnp.where(lane_g == g, vb, 0.0) for g in range(SGU_GROUPS)], axis=0
        ).astype(_BF16)
        mixed.append(_dot(w_cat, rhs) + sgu_b_ref[...])
    o_c = (s_u * jnp.concatenate(mixed, axis=0)).astype(_BF16)
    merged += _sigmoid(_dot(hb, w_gate_ref[:, 2 * D_MODEL:3 * D_MODEL])) * _dot(o_c, w_brc_ref[...])

    zc = small[:, _OFF_CVC:_OFF_CVC + CONV_WIDTH] * small[:, _OFF_CVX:_OFF_CVX + CONV_WIDTH]
    zcext = jnp.concatenate([conv_halo[...], zc], axis=0)
    conv_halo[...] = zc[T - CONV_HALO:, :]
    y = (conv_w_ref[0:1, :] * pltpu.roll(zcext, 2, 0)[CONV_HALO:]
         + conv_w_ref[1:2, :] * pltpu.roll(zcext, 1, 0)[CONV_HALO:]
         + conv_w_ref[2:3, :] * zc)
    o_d = (small[:, _OFF_CVB:_OFF_CVB + CONV_WIDTH] * y).astype(_BF16)
    merged += _sigmoid(_dot(hb, w_gate_ref[:, 3 * D_MODEL:4 * D_MODEL])) * _dot(o_d, w_brd_ref[...])

    out = _dot(merged.astype(_BF16), w_out_ref[...])
    o_ref[...] = x + _rms(out, g_post_ref[...])


def _const_spec(shape):
    nd = len(shape)
    return pl.BlockSpec(shape, lambda b, t: (0,) * nd, pipeline_mode=pl.Buffered(1))


def _mix_layer(x, ct, st, p):
    B, S, D = x.shape
    T = MIX_TILE
    weights = [p["g_pre_mix"], p["w_small"], p["w_gate"], p["g_cq"], p["g_ckv"], p["w_uq"],
               p["w_uk"], p["w_uvt"], p["pool_w"], p["pool_scale"], p["g_sgu_v"], p["sgu_w"], p["sgu_b"],
               p["conv_w"], p["w_br_a"], p["w_br_b"], p["w_br_c"], p["w_br_d"], p["w_out"],
               p["g_post_mix"]]
    x_spec = pl.BlockSpec((None, T, D), lambda b, t: (b, t, 0))
    tab_spec = pl.BlockSpec((T, LANES), lambda b, t: (t, 0))
    return pl.pallas_call(
        _mix_kernel,
        out_shape=jax.ShapeDtypeStruct(x.shape, x.dtype),
        grid=(B, S // T),
        in_specs=[x_spec, tab_spec, tab_spec] + [_const_spec(w.shape) for w in weights],
        out_specs=x_spec,
        scratch_shapes=[
            pltpu.VMEM((S // T, MLA_HEADS, T, HEAD_PAD), _BF16),
            pltpu.VMEM((S // T, MLA_HEADS, HEAD_PAD, T), _BF16),
            pltpu.VMEM((MLA_HEADS, HEAD_PAD, T), _BF16),
            pltpu.VMEM((MLA_HEADS, T, T), _F32),
            pltpu.VMEM((MLA_HEADS, 1, T), _F32),
            pltpu.VMEM((MLA_HEADS, HEAD_PAD, T), _F32),
            pltpu.VMEM((POOL_MAX_WIN, POOL_WIDTH), _F32),
            pltpu.VMEM((CONV_HALO, CONV_WIDTH), _F32),
        ],
        compiler_params=pltpu.CompilerParams(
            dimension_semantics=("arbitrary", "arbitrary"),
            vmem_limit_bytes=VMEM_LIMIT_BYTES),
        name="mix_sublayer",
    )(x, ct, st, *weights)


def _ffn_kernel(x_ref, g_pre_ref, w_gate_ref, w_up_ref, w_down_ref, g_post_ref, o_ref):
    x = x_ref[...]
    hb = _rms(x, g_pre_ref[...]).astype(_BF16)
    a = _dot(hb, w_gate_ref[...])
    u = _dot(hb, w_up_ref[...])
    f = (a * _sigmoid(a) * u).astype(_BF16)
    o_ref[...] = x + _rms(_dot(f, w_down_ref[...]), g_post_ref[...])


def _ffn_layer(x2, p):
    N, D = x2.shape
    T = FFN_TILE
    weights = [p["g_pre_ffn"], p["w_ffn_gate"], p["w_ffn_up"], p["w_ffn_down"], p["g_post_ffn"]]
    x_spec = pl.BlockSpec((T, D), lambda i: (i, 0))

    def const_spec(shape):
        nd = len(shape)
        return pl.BlockSpec(shape, lambda i: (0,) * nd, pipeline_mode=pl.Buffered(1))

    return pl.pallas_call(
        _ffn_kernel,
        out_shape=jax.ShapeDtypeStruct(x2.shape, x2.dtype),
        grid=(N // T,),
        in_specs=[x_spec] + [const_spec(w.shape) for w in weights],
        out_specs=x_spec,
        compiler_params=pltpu.CompilerParams(
            dimension_semantics=("arbitrary",),
            vmem_limit_bytes=VMEM_LIMIT_BYTES),
        name="ffn_sublayer",
    )(x2, *weights)


def _rope_tables(seq):
    inv = ROPE_THETA ** (-jnp.arange(0, QK_ROPE, 2, dtype=_F32) / QK_ROPE)
    ang = jnp.arange(seq, dtype=_F32)[:, None] * inv[None, :]
    cos, sin = jnp.cos(ang), jnp.sin(ang)
    ones = jnp.ones((seq, QK_NOPE), _F32)
    zeros_n = jnp.zeros((seq, QK_NOPE), _F32)
    zeros_p = jnp.zeros((seq, HEAD_PAD - QK_NOPE - QK_ROPE), _F32)
    ct = jnp.concatenate([ones, cos, cos, zeros_p], axis=-1)
    st = jnp.concatenate([zeros_n, -sin, sin, zeros_p], axis=-1)
    return ct, st


def _swap_halves(w):
    half = w.shape[-1] // 2
    return jnp.concatenate([w[..., half:], w[..., :half]], axis=-1)


def _prepare(w_in, g_pre_mix, g_cq, g_ckv, w_uq, w_ukv, pool_w, pool_scale, g_sgu_v, sgu_w,
             sgu_b, conv_w, w_br_a, w_br_b, w_br_c, w_br_d, w_out, g_post_mix, g_pre_ffn,
             w_ffn_gate, w_ffn_up, w_ffn_down, g_post_ffn):
    L = w_in.shape[0]
    pad = HEAD_PAD - QK_NOPE - QK_ROPE

    o = Q_LORA + KV_LORA
    kr = w_in[..., o:o + QK_ROPE]
    zn = jnp.zeros(kr.shape[:-1] + (QK_NOPE,), kr.dtype)
    zp = jnp.zeros(kr.shape[:-1] + (pad,), kr.dtype)
    n_small = o + QK_ROPE
    n_rest = n_small + POOL_WIDTH + 2 * SGU_WIDTH + 3 * CONV_WIDTH
    w_small = jnp.concatenate(
        [w_in[..., :o], zn, kr, zp, zn, _swap_halves(kr), zp, w_in[..., n_small:n_rest]], axis=-1)
    assert w_small.shape[-1] == SMALL_WIDTH
    w_gate = w_in[..., n_rest:]

    uq = w_uq.reshape(L, Q_LORA, MLA_HEADS, QK_NOPE + QK_ROPE)
    uq_n, uq_r = uq[..., :QK_NOPE], uq[..., QK_NOPE:]
    zq = jnp.zeros(uq_r.shape[:-1] + (pad,), uq.dtype)
    q_heads = jnp.concatenate([uq_n, uq_r, zq], axis=-1).reshape(L, Q_LORA, MLA_HEADS * HEAD_PAD)
    q_swap = jnp.concatenate([jnp.zeros_like(uq_n), _swap_halves(uq_r), zq], axis=-1
                             ).reshape(L, Q_LORA, MLA_HEADS * HEAD_PAD)
    w_uq_all = jnp.concatenate([q_heads, q_swap], axis=-1)

    ukv = w_ukv.reshape(L, KV_LORA, MLA_HEADS, QK_NOPE + V_HEAD)
    uk, uv = ukv[..., :QK_NOPE], ukv[..., QK_NOPE:]
    k_heads = jnp.concatenate([uk, jnp.zeros(uk.shape[:-1] + (HEAD_PAD - QK_NOPE,), uk.dtype)],
                              axis=-1).reshape(L, KV_LORA, MLA_HEADS * HEAD_PAD)
    v_heads = jnp.concatenate([uv, jnp.zeros(uv.shape[:-1] + (HEAD_PAD - V_HEAD,), uv.dtype)],
                              axis=-1).reshape(L, KV_LORA, MLA_HEADS * HEAD_PAD)
    w_uvt = jnp.swapaxes(v_heads, 1, 2)

    pool_bd = jnp.zeros((L, POOL_WIDTH, POOL_WIDTH), pool_w.dtype)
    for g in range(len(POOL_WINDOWS)):
        sl = slice(g * POOL_GROUP_DIM, (g + 1) * POOL_GROUP_DIM)
        pool_bd = pool_bd.at[:, sl, sl].set(pool_w[:, g])

    sgu_bias = jnp.repeat(jnp.swapaxes(sgu_b, 1, 2), SGU_GROUP_DIM, axis=-1)

    def row(g):
        return g[:, None, :]

    bf = lambda w: w.astype(_BF16)
    stacked = {
        "g_pre_mix": row(g_pre_mix), "w_small": bf(w_small), "w_gate": bf(w_gate),
        "g_cq": row(g_cq), "g_ckv": row(g_ckv), "w_uq": bf(w_uq_all), "w_uk": bf(k_heads),
        "w_uvt": bf(w_uvt),
        "pool_w": bf(pool_bd), "pool_scale": row(pool_scale), "g_sgu_v": row(g_sgu_v),
        "sgu_w": sgu_w, "sgu_b": sgu_bias, "conv_w": conv_w[:, :, 0, :],
        "w_br_a": bf(w_br_a), "w_br_b": bf(w_br_b), "w_br_c": bf(w_br_c), "w_br_d": bf(w_br_d),
        "w_out": bf(w_out), "g_post_mix": row(g_post_mix), "g_pre_ffn": row(g_pre_ffn),
        "w_ffn_gate": bf(w_ffn_gate), "w_ffn_up": bf(w_ffn_up), "w_ffn_down": bf(w_ffn_down),
        "g_post_ffn": row(g_post_ffn),
    }
    return [{k: v[l] for k, v in stacked.items()} for l in range(L)]


def kernel(x, w_in, g_pre_mix, g_cq, g_ckv, w_uq, w_ukv, pool_w, pool_scale, g_sgu_v, sgu_w, sgu_b, conv_w, w_br_a, w_br_b, w_br_c, w_br_d, w_out, g_post_mix, g_pre_ffn, w_ffn_gate, w_ffn_up, w_ffn_down, g_post_ffn):
    B, S, D = x.shape
    layers = _prepare(w_in, g_pre_mix, g_cq, g_ckv, w_uq, w_ukv, pool_w, pool_scale, g_sgu_v,
                      sgu_w, sgu_b, conv_w, w_br_a, w_br_b, w_br_c, w_br_d, w_out, g_post_mix,
                      g_pre_ffn, w_ffn_gate, w_ffn_up, w_ffn_down, g_post_ffn)
    ct, st = _rope_tables(S)
    for p in layers:
        x = _mix_layer(x, ct, st, p)
        x = _ffn_layer(x.reshape(B * S, D), p).reshape(B, S, D)
    return x
```
